```python
import math
import jax, jax.numpy as jnp
from jax import lax
import numpy as np

D_MODEL = 1024
BATCH = 8
SEQ = 4096
DEPTH = 2

CTX_LEN = 256
GRID_W = 64
NORM_EPS = 1e-6
CHUNK = 64
N_BRANCHES = 3
NEG_BIG = -1e30
TINY = 1e-20

S5_GROUPS = 16
S5_GROUP_CH = 16
S5_WIDTH = S5_GROUPS * S5_GROUP_CH
S5_STATE = 64
S5_DT_MIN = 1e-3
S5_DT_MAX = 1e-1

GLA_HEADS = 4
GLA_DK = 64
GLA_DV = 128
GLA_KW = GLA_HEADS * GLA_DK
GLA_VW = GLA_HEADS * GLA_DV
GLA_GATE_RANK = 16
GLA_GATE_NORM = 16.0

HGRN_HEADS = 4
HGRN_EXPAND = 64
HGRN_DV = 64
HGRN_KW = HGRN_HEADS * HGRN_EXPAND
HGRN_VW = HGRN_HEADS * HGRN_DV

N_EXPERTS = 16
N_EXPERT_GROUPS = 4
EXPERTS_PER_GROUP = N_EXPERTS // N_EXPERT_GROUPS
TOP_K = 2
EXPERT_FF = 512

IN_SPLITS = (S5_WIDTH, GLA_KW, GLA_KW, GLA_VW, GLA_VW, 2 * GLA_GATE_RANK, HGRN_KW, 2 * HGRN_KW, HGRN_VW, HGRN_VW, N_BRANCHES * D_MODEL)
IN_WIDTH = S5_WIDTH + 2 * GLA_KW + 2 * GLA_VW + 2 * GLA_GATE_RANK + 3 * HGRN_KW + 2 * HGRN_VW + N_BRANCHES * D_MODEL

kernel_name = 'hybrid_s5_gla_hgrn2_moe_dit'


def _rmsnorm(t, g):
    tf = t.astype(jnp.float32)
    y = tf * lax.rsqrt(jnp.mean(tf * tf, axis=-1, keepdims=True) + NORM_EPS)
    return (y * g.astype(jnp.float32)).astype(t.dtype)


def _modulate(t, shift, scale):
    return t * (1 + scale) + shift


def _split_cols(t, sizes):
    out, start = [], 0
    for s in sizes:
        out.append(t[..., start:start + s])
        start += s
    return out


def _flip_parts(t, n_ctx):
    return jnp.concatenate([jnp.flip(t[:, :n_ctx], axis=1), jnp.flip(t[:, n_ctx:], axis=1)], axis=1)


def _latent_to_cols(t, n_ctx, rows):
    lat = t[:, n_ctx:]
    b, tail = lat.shape[0], lat.shape[2:]
    lat = lat.reshape((b, rows, GRID_W) + tail).swapaxes(1, 2).reshape((b, rows * GRID_W) + tail)
    return jnp.concatenate([t[:, :n_ctx], lat], axis=1)


def _latent_to_rows(t, n_ctx, rows):
    lat = t[:, n_ctx:]
    b, tail = lat.shape[0], lat.shape[2:]
    lat = lat.reshape((b, GRID_W, rows) + tail).swapaxes(1, 2).reshape((b, rows * GRID_W) + tail)
    return jnp.concatenate([t[:, :n_ctx], lat], axis=1)


def _chunk_gla(q, k, v, log_f):
    b, n, h, dk = q.shape
    dv = v.shape[-1]
    nc = n // CHUNK

    def chunks(t):
        return t.astype(jnp.float32).reshape(b, nc, CHUNK, h, t.shape[-1]).transpose(1, 0, 3, 2, 4)

    causal = jnp.tril(jnp.ones((CHUNK, CHUNK), dtype=bool))

    def step(state, inp):
        qc, kc, vc, gc = inp
        cum = jnp.cumsum(gc, axis=2)
        o_inter = jnp.einsum('bhck,bhkv->bhcv', qc * jnp.exp(cum), state)
        rel = cum[:, :, :, None, :] - cum[:, :, None, :, :]
        rel = jnp.where(causal[:, :, None], rel, NEG_BIG)
        scores = jnp.einsum('bhtk,bhsk,bhtsk->bhts', qc, kc, jnp.exp(rel))
        o_intra = jnp.einsum('bhts,bhsv->bhtv', scores, vc)
        last = cum[:, :, -1:, :]
        new_state = (jnp.exp(last[:, :, 0, :, None]) * state
                     + jnp.einsum('bhck,bhcv->bhkv', kc * jnp.exp(last - cum), vc))
        return new_state, o_inter + o_intra

    s0 = jnp.zeros((b, h, dk, dv), jnp.float32)
    _, o = lax.scan(step, s0, (chunks(q), chunks(k), chunks(v), chunks(log_f)))
    return o.transpose(1, 0, 3, 2, 4).reshape(b, n, h, dv)


def _bidir_gla(q, k_f, k_b, v, g_f, g_b, n_ctx):
    o_f = _chunk_gla(q, k_f, v, g_f)
    o_b = _chunk_gla(_flip_parts(q, n_ctx), _flip_parts(k_b, n_ctx), _flip_parts(v, n_ctx), _flip_parts(g_b, n_ctx))
    return o_f + _flip_parts(o_b, n_ctx)


def _s5_discretize(lam_re, lam_im, log_step, b_re, b_im):
    f32 = jnp.float32
    lam_re, lam_im = lam_re.astype(f32), lam_im.astype(f32)
    dt = jnp.exp(log_step.astype(f32))[:, None]
    mag = jnp.exp(lam_re * dt)
    ab_re, ab_im = mag * jnp.cos(lam_im * dt), mag * jnp.sin(lam_im * dt)
    den = lam_re * lam_re + lam_im * lam_im
    nr, ni = ab_re - 1.0, ab_im
    fr = (nr * lam_re + ni * lam_im) / den
    fi = (ni * lam_re - nr * lam_im) / den
    b_re, b_im = b_re.astype(f32), b_im.astype(f32)
    bb_re = fr[..., None] * b_re - fi[..., None] * b_im
    bb_im = fr[..., None] * b_im + fi[..., None] * b_re
    return ab_re, ab_im, bb_re, bb_im


def _s5_scan(u, ab_re, ab_im, bb_re, bb_im):
    n = u.shape[1]
    bu_re = jnp.einsum('bngh,gph->bngp', u, bb_re)
    bu_im = jnp.einsum('bngh,gph->bngp', u, bb_im)
    a_re = jnp.broadcast_to(ab_re, (1, n) + ab_re.shape)
    a_im = jnp.broadcast_to(ab_im, (1, n) + ab_im.shape)

    def combine(e1, e2):
        ar1, ai1, br1, bi1 = e1
        ar2, ai2, br2, bi2 = e2
        return (ar2 * ar1 - ai2 * ai1, ar2 * ai1 + ai2 * ar1,
                ar2 * br1 - ai2 * bi1 + br2, ar2 * bi1 + ai2 * br1 + bi2)

    _, _, x_re, x_im = lax.associative_scan(combine, (a_re, a_im, bu_re, bu_im), axis=1)
    return x_re, x_im


def _s5_mixer(u, n_ctx, lam_re, lam_im, log_step, b_re, b_im, c_re, c_im, d_skip, glu_w, glu_b):
    f32 = jnp.float32
    bsz, n, _ = u.shape
    uf = u.astype(f32)
    ug = uf.reshape(bsz, n, S5_GROUPS, S5_GROUP_CH)
    fwd_re, fwd_im = _s5_scan(ug, *_s5_discretize(lam_re[0], lam_im[0], log_step[0], b_re, b_im))
    bwd_re, bwd_im = _s5_scan(_flip_parts(ug, n_ctx), *_s5_discretize(lam_re[1], lam_im[1], log_step[1], b_re, b_im))
    x_re = fwd_re + _flip_parts(bwd_re, n_ctx)
    x_im = fwd_im + _flip_parts(bwd_im, n_ctx)
    y = (jnp.einsum('bngp,ghp->bngh', x_re, c_re.astype(f32))
         - jnp.einsum('bngp,ghp->bngh', x_im, c_im.astype(f32)))
    y = y.reshape(bsz, n, S5_WIDTH) + d_skip.astype(f32) * uf
    y = jax.nn.gelu(y)
    y = y * jax.nn.sigmoid(y @ glu_w.astype(f32) + glu_b.astype(f32))
    return y.astype(u.dtype)


def _gla_mixer(q, k, v, r, code, gate_up, gate_b, norm_g, n_ctx):
    f32 = jnp.float32
    bsz, n, _ = q.shape

    def heads(t, d):
        return t.astype(f32).reshape(bsz, n, GLA_HEADS, d)

    def log_gate(cd, w, bias):
        return jax.nn.log_sigmoid((cd @ w + bias).astype(f32)) / GLA_GATE_NORM

    qh = heads(q, GLA_DK) * (GLA_DK ** -0.5)
    kh = heads(k, GLA_DK)
    vh = heads(v, GLA_DV)
    g_f = heads(log_gate(code[..., :GLA_GATE_RANK], gate_up[0], gate_b[0]), GLA_DK)
    g_b = heads(log_gate(code[..., GLA_GATE_RANK:], gate_up[1], gate_b[1]), GLA_DK)
    o = _bidir_gla(qh, kh, kh, vh, g_f, g_b, n_ctx)
    o = _rmsnorm(o, norm_g).reshape(bsz, n, GLA_VW)
    return (o * jax.nn.silu(r.astype(f32))).astype(q.dtype)


def _hgrn2_mixer(q_raw, f_raw, i_in, g_out, lb, norm_g, n_ctx, rows):
    f32 = jnp.float32
    bsz, n, _ = q_raw.shape
    lbf = lb.astype(f32)

    def gates(fr):
        fr = fr.astype(f32)
        f = lbf + (1.0 - lbf) * jax.nn.sigmoid(fr)
        log_f = jnp.log(jnp.maximum(f, TINY))
        k = (1.0 - lbf) * jax.nn.sigmoid(-fr)
        return log_f, k

    def prep(t, d):
        return _latent_to_cols(t.reshape(bsz, n, HGRN_HEADS, d), n_ctx, rows)

    log_f_f, k_f = gates(f_raw[..., :HGRN_KW])
    log_f_b, k_b = gates(f_raw[..., HGRN_KW:])
    q = jax.nn.silu(q_raw.astype(f32))
    o = _bidir_gla(prep(q, HGRN_EXPAND), prep(k_f, HGRN_EXPAND), prep(k_b, HGRN_EXPAND),
                   prep(i_in.astype(f32), HGRN_DV), prep(log_f_f, HGRN_EXPAND), prep(log_f_b, HGRN_EXPAND), n_ctx)
    o = _latent_to_rows(o, n_ctx, rows)
    o = _rmsnorm(o, norm_g).reshape(bsz, n, HGRN_VW)
    return (o * jax.nn.silu(g_out.astype(f32))).astype(q_raw.dtype)


def _moe(h, router_w, router_b, w_up, w_down):
    f32 = jnp.float32
    probs = jax.nn.softmax((h @ router_w).astype(f32), axis=-1)
    sel = probs + router_b.astype(f32)
    grouped = sel.reshape(sel.shape[:-1] + (N_EXPERT_GROUPS, EXPERTS_PER_GROUP))
    group_score = lax.top_k(grouped, TOP_K)[0].sum(axis=-1)
    best = jnp.argmax(group_score, axis=-1)
    in_best = jnp.arange(N_EXPERT_GROUPS) == best[..., None]
    masked = jnp.where(in_best[..., None], grouped, NEG_BIG).reshape(sel.shape)
    _, idx = lax.top_k(masked, TOP_K)
    w_sel = jnp.take_along_axis(probs, idx, axis=-1)
    w_sel = w_sel / jnp.sum(w_sel, axis=-1, keepdims=True)
    gates = jnp.einsum('bnk,bnke->bne', w_sel, jax.nn.one_hot(idx, N_EXPERTS, dtype=f32))
    out = jnp.zeros(h.shape, f32)
    for e in range(N_EXPERTS):
        gu = h @ w_up[e]
        hid = jax.nn.silu(gu[..., :EXPERT_FF]) * gu[..., EXPERT_FF:]
        out = out + gates[..., e:e + 1] * (hid @ w_down[e]).astype(f32)
    return out.astype(h.dtype)


def setup_inputs(seed: int = 0) -> dict:
    key = jax.random.key(seed)
    ks = iter(jax.random.split(key, 40))
    f32 = jnp.float32
    D = D_MODEL

    def nrm(shape, scale):
        return jax.random.normal(next(ks), shape, f32) * scale

    x = nrm((BATCH, SEQ, D), 1.0)
    c = nrm((BATCH, D), 1.0)
    ctx = nrm((BATCH, CTX_LEN, D), 1.0)
    c_ctx = nrm((D,), 1.0)
    w_mod = nrm((DEPTH, D, 6 * D), 0.5 * D ** -0.5)
    b_mod = nrm((DEPTH, 6 * D), 0.02)
    norm_mix_g = 1.0 + nrm((DEPTH, D), 0.02)
    norm_ffn_g = 1.0 + nrm((DEPTH, D), 0.02)
    w_in = nrm((DEPTH, D, IN_WIDTH), D ** -0.5)
    n_idx = jnp.arange(S5_STATE, dtype=f32)
    s5_lam_re = -0.5 + nrm((DEPTH, 2, S5_GROUPS, S5_STATE), 0.01)
    s5_lam_im = math.pi * n_idx + nrm((DEPTH, 2, S5_GROUPS, S5_STATE), 0.01)
    s5_log_step = jax.random.uniform(next(ks), (DEPTH, 2, S5_GROUPS), f32,
                                     minval=math.log(S5_DT_MIN), maxval=math.log(S5_DT_MAX))
    s5_b_re = nrm((DEPTH, S5_GROUPS, S5_STATE, S5_GROUP_CH), (2 * S5_GROUP_CH) ** -0.5)
    s5_b_im = nrm((DEPTH, S5_GROUPS, S5_STATE, S5_GROUP_CH), (2 * S5_GROUP_CH) ** -0.5)
    s5_c_re = nrm((DEPTH, S5_GROUPS, S5_GROUP_CH, S5_STATE), 2.0 * S5_STATE ** -0.5)
    s5_c_im = nrm((DEPTH, S5_GROUPS, S5_GROUP_CH, S5_STATE), 2.0 * S5_STATE ** -0.5)
    s5_d = nrm((DEPTH, S5_WIDTH), 0.5)
    s5_glu_w = nrm((DEPTH, S5_WIDTH, S5_WIDTH), S5_WIDTH ** -0.5)
    s5_glu_b = nrm((DEPTH, S5_WIDTH), 0.02)
    gla_gate_up = nrm((DEPTH, 2, GLA_GATE_RANK, GLA_KW), GLA_GATE_RANK ** -0.5)
    gla_gate_b = nrm((DEPTH, 2, GLA_KW), 0.1)
    gla_norm_g = 1.0 + nrm((DEPTH, GLA_DV), 0.02)
    hgrn_lower = nrm((DEPTH, HGRN_KW), 0.5)
    hgrn_norm_g = 1.0 + nrm((DEPTH, HGRN_DV), 0.02)
    w_branch_s5 = nrm((DEPTH, S5_WIDTH, D), S5_WIDTH ** -0.5)
    w_branch_gla = nrm((DEPTH, GLA_VW, D), GLA_VW ** -0.5)
    w_branch_hgrn = nrm((DEPTH, HGRN_VW, D), HGRN_VW ** -0.5)
    w_out = nrm((DEPTH, D, D), D ** -0.5)
    router_w = nrm((D, N_EXPERTS), D ** -0.5)
    router_b = nrm((N_EXPERTS,), 0.01)
    moe_w_up = nrm((DEPTH, N_EXPERTS, D, 2 * EXPERT_FF), D ** -0.5)
    moe_w_down = nrm((DEPTH, N_EXPERTS, EXPERT_FF, D), EXPERT_FF ** -0.5)
    final_norm_g = 1.0 + nrm((D,), 0.02)
    return {'x': x, 'c': c, 'ctx': ctx, 'c_ctx': c_ctx, 'w_mod': w_mod, 'b_mod': b_mod,
            'norm_mix_g': norm_mix_g, 'norm_ffn_g': norm_ffn_g, 'w_in': w_in,
            's5_lam_re': s5_lam_re, 's5_lam_im': s5_lam_im, 's5_log_step': s5_log_step,
            's5_b_re': s5_b_re, 's5_b_im': s5_b_im, 's5_c_re': s5_c_re, 's5_c_im': s5_c_im,
            's5_d': s5_d, 's5_glu_w': s5_glu_w, 's5_glu_b': s5_glu_b,
            'gla_gate_up': gla_gate_up, 'gla_gate_b': gla_gate_b, 'gla_norm_g': gla_norm_g,
            'hgrn_lower': hgrn_lower, 'hgrn_norm_g': hgrn_norm_g,
            'w_branch_s5': w_branch_s5, 'w_branch_gla': w_branch_gla, 'w_branch_hgrn': w_branch_hgrn,
            'w_out': w_out, 'router_w': router_w, 'router_b': router_b,
            'moe_w_up': moe_w_up, 'moe_w_down': moe_w_down, 'final_norm_g': final_norm_g}


def reference(x, c, ctx, c_ctx, w_mod, b_mod, norm_mix_g, norm_ffn_g, w_in,
              s5_lam_re, s5_lam_im, s5_log_step, s5_b_re, s5_b_im, s5_c_re, s5_c_im,
              s5_d, s5_glu_w, s5_glu_b, gla_gate_up, gla_gate_b, gla_norm_g,
              hgrn_lower, hgrn_norm_g, w_branch_s5, w_branch_gla, w_branch_hgrn, w_out,
              router_w, router_b, moe_w_up, moe_w_down, final_norm_g):
    n_ctx = ctx.shape[1]
    rows = x.shape[1] // GRID_W
    p_lb = jax.nn.softmax(hgrn_lower.astype(jnp.float32), axis=0)
    lower_bounds = jnp.cumsum(p_lb, axis=0) - p_lb[0]
    silu_c = jax.nn.silu(c)
    silu_cc = jax.nn.silu(c_ctx)
    h_ctx, h_lat = ctx, x
    for layer in range(DEPTH):
        last = layer == DEPTH - 1
        m_lat = jnp.split((silu_c @ w_mod[layer] + b_mod[layer])[:, None, :], 6, axis=-1)
        m_ctx = jnp.split(silu_cc @ w_mod[layer] + b_mod[layer], 6, axis=-1)
        u = jnp.concatenate([
            _modulate(_rmsnorm(h_ctx, norm_mix_g[layer]), m_ctx[0], m_ctx[1]),
            _modulate(_rmsnorm(h_lat, norm_mix_g[layer]), m_lat[0], m_lat[1])], axis=1)
        (u_s5, q_gla, k_gla, v_gla, r_gla, code_gla,
         q_hg, f_hg, i_hg, o_gate_hg, gate_logits) = _split_cols(u @ w_in[layer], IN_SPLITS)
        y_s5 = _s5_mixer(u_s5, n_ctx, s5_lam_re[layer], s5_lam_im[layer], s5_log_step[layer],
                         s5_b_re[layer], s5_b_im[layer], s5_c_re[layer], s5_c_im[layer],
                         s5_d[layer], s5_glu_w[layer], s5_glu_b[layer])
        y_gla = _gla_mixer(q_gla, k_gla, v_gla, r_gla, code_gla, gla_gate_up[layer], gla_gate_b[layer],
                           gla_norm_g[layer], n_ctx)
        y_hg = _hgrn2_mixer(q_hg, f_hg, i_hg, o_gate_hg, lower_bounds[layer], hgrn_norm_g[layer], n_ctx, rows)
        keep = n_ctx if last else 0
        gate_a, gate_b, gate_c = jnp.split(jax.nn.sigmoid(gate_logits[:, keep:]), N_BRANCHES, axis=-1)
        merged = (gate_a * (y_s5[:, keep:] @ w_branch_s5[layer])
                  + gate_b * (y_gla[:, keep:] @ w_branch_gla[layer])
                  + gate_c * (y_hg[:, keep:] @ w_branch_hgrn[layer]))
        mix = merged @ w_out[layer]
        if last:
            h_lat = h_lat + m_lat[2] * mix
            v_in = _modulate(_rmsnorm(h_lat, norm_ffn_g[layer]), m_lat[3], m_lat[4])
            h_lat = h_lat + m_lat[5] * _moe(v_in, router_w, router_b, moe_w_up[layer], moe_w_down[layer])
        else:
            h_ctx = h_ctx + m_ctx[2] * mix[:, :n_ctx]
            h_lat = h_lat + m_lat[2] * mix[:, n_ctx:]
            v_in = jnp.concatenate([
                _modulate(_rmsnorm(h_ctx, norm_ffn_g[layer]), m_ctx[3], m_ctx[4]),
                _modulate(_rmsnorm(h_lat, norm_ffn_g[layer]), m_lat[3], m_lat[4])], axis=1)
            ff = _moe(v_in, router_w, router_b, moe_w_up[layer], moe_w_down[layer])
            h_ctx = h_ctx + m_ctx[5] * ff[:, :n_ctx]
            h_lat = h_lat + m_lat[5] * ff[:, n_ctx:]
    return _rmsnorm(h_lat, final_norm_g)
```

```python
import functools
import math

import jax
import jax.numpy as jnp
from jax import lax
from jax.experimental import pallas as pl
from jax.experimental.pallas import tpu as pltpu

F32 = jnp.float32
BF16 = jnp.bfloat16

D_MODEL = 1024
GRID_W = 64
NORM_EPS = 1e-6
CHUNK = 64
TINY = 1e-20

S5_GROUPS = 16
S5_GROUP_CH = 16
S5_WIDTH = S5_GROUPS * S5_GROUP_CH

GLA_HEADS = 4
GLA_DK = 64
GLA_DV = 128
GLA_KW = GLA_HEADS * GLA_DK
GLA_VW = GLA_HEADS * GLA_DV
GLA_GATE_RANK = 16
GLA_GATE_NORM = 16.0

HGRN_HEADS = 4
HGRN_EXPAND = 64
HGRN_DV = 64
HGRN_KW = HGRN_HEADS * HGRN_EXPAND
HGRN_VW = HGRN_HEADS * HGRN_DV

N_EXPERTS = 16
N_EXPERT_GROUPS = 4
EXPERTS_PER_GROUP = N_EXPERTS // N_EXPERT_GROUPS
EXPERT_FF = 512

LANES = 128
MOD_ROWS = 8
VMEM_LIMIT = 56 * 1024 * 1024

COL_R = 0
COL_OG = 512
COL_U = 768
COL_GATES = 1024
COL_GLA = 4096
COL_HG = 5120
COL_CODE = 6144
ACT_W = 6272
MERGE_W = 4096

SCAN_SUB = 16
SCAN_BLOCK = 256
EXP_CLAMP = 60.0
TOKEN_TILE = 512
MOE_TILE = 1024


def _dot(a, b):
    return jnp.dot(a, b, preferred_element_type=F32)


def _dot_nt(a, b):
    return lax.dot_general(a, b, (((1,), (1,)), ((), ())), preferred_element_type=F32)


def _dot_tn(a, b):
    return lax.dot_general(a, b, (((0,), (0,)), ((), ())), preferred_element_type=F32)


def _split(x):
    hi = x.astype(BF16)
    lo = (x - hi.astype(F32)).astype(BF16)
    return hi, lo


def _dot3(a, b, dot=_dot):
    ah, al = _split(a)
    bh, bl = _split(b)
    return dot(ah, bh) + (dot(ah, bl) + dot(al, bh))


def _sigmoid(x):
    return 1.0 / (1.0 + jnp.exp(-x))


def _silu(x):
    return x * _sigmoid(x)


def _iota(shape, axis):
    return lax.broadcasted_iota(jnp.int32, shape, axis)


def _params(*sem):
    return pltpu.CompilerParams(dimension_semantics=sem, vmem_limit_bytes=VMEM_LIMIT)


def _rms(x, g):
    ms = jnp.mean(x * x, axis=-1, keepdims=True)
    return x * lax.rsqrt(ms + NORM_EPS) * g


class _Tokens:
    def __init__(self, bsz, n_ctx, seq):
        self.bsz, self.n_ctx, self.seq = bsz, n_ctx, seq
        self.total = bsz * (n_ctx + seq)

    def mod_index(self, tile, tm):
        assert (self.bsz * self.n_ctx) % tm == 0 and self.seq % tm == 0
        ctx_tiles = self.bsz * self.n_ctx // tm
        return jnp.where(tile < ctx_tiles, self.bsz, (tile - ctx_tiles) // (self.seq // tm))

    def mod_spec(self, tm, d):
        return pl.BlockSpec((1, MOD_ROWS, d), lambda j, *_: (self.mod_index(j, tm), 0, 0))

    def seq_block(self, b, blk, block):
        cb, lb = self.n_ctx // block, self.seq // block
        return jnp.where(blk < cb, b * cb + blk, self.bsz * cb + b * lb + (blk - cb))

    def to_sequences(self, t):
        w = t.shape[-1]
        split = self.bsz * self.n_ctx
        return jnp.concatenate([t[:split].reshape(self.bsz, self.n_ctx, w),
                                t[split:].reshape(self.bsz, self.seq, w)], axis=1)

    def from_sequences(self, t):
        w = t.shape[-1]
        return jnp.concatenate([t[:, :self.n_ctx].reshape(-1, w), t[:, self.n_ctx:].reshape(-1, w)], axis=0)


def _mod_kernel(c_ref, w_ref, b_ref, o_ref):
    o_ref[0] = _dot3(_silu(c_ref[...]), w_ref[0]) + b_ref[0]


def _mod_call(cc, w_mod, b_mod):
    depth, d, wtot = w_mod.shape
    tn = wtot // 4
    return pl.pallas_call(
        _mod_kernel,
        grid=(depth, wtot // tn),
        in_specs=[
            pl.BlockSpec(cc.shape, lambda l, j: (0, 0)),
            pl.BlockSpec((1, d, tn), lambda l, j: (l, 0, j)),
            pl.BlockSpec((1, 1, tn), lambda l, j: (l, 0, j)),
        ],
        out_specs=pl.BlockSpec((1, cc.shape[0], tn), lambda l, j: (l, 0, j)),
        out_shape=jax.ShapeDtypeStruct((depth, cc.shape[0], wtot), F32),
        compiler_params=_params("parallel", "parallel"),
        name="mod",
    )(cc, w_mod, b_mod.reshape(depth, 1, wtot))


def _proj_in_kernel(h_ref, m_ref, g_ref, w_ref, o_ref, *, col_chunk):
    m = m_ref[0]
    u = (_rms(h_ref[...], g_ref[...]) * (1.0 + m[1:2]) + m[0:1]).astype(BF16)
    for c0 in range(0, w_ref.shape[1], col_chunk):
        o_ref[:, c0:c0 + col_chunk] = _dot(u, w_ref[:, c0:c0 + col_chunk]).astype(BF16)


def _proj_in_call(tk, h, mods, g, w, *, tm):
    t, d = h.shape
    wtot = w.shape[1]
    return pl.pallas_call(
        functools.partial(_proj_in_kernel, col_chunk=wtot // 7),
        grid=(t // tm,),
        in_specs=[
            pl.BlockSpec((tm, d), lambda j: (j, 0)),
            tk.mod_spec(tm, d),
            pl.BlockSpec((1, d), lambda j: (0, 0)),
            pl.BlockSpec((d, wtot), lambda j: (0, 0), pipeline_mode=pl.Buffered(1)),
        ],
        out_specs=pl.BlockSpec((tm, wtot), lambda j: (j, 0)),
        out_shape=jax.ShapeDtypeStruct((t, wtot), BF16),
        compiler_params=_params("parallel"),
        name="proj_in",
    )(h, mods, g, w)


def _s5_prep_kernel(p_ref, btr_ref, bti_ref, cr_ref, ci_ref,
                    wre_ref, wim_ref, ere_ref, eim_ref, k_ref, apr_ref, api_ref, *, chunk):
    p = p_ref[0]
    btr, bti = btr_ref[0], bti_ref[0]
    cr, ci = cr_ref[0], ci_ref[0]
    for dr in range(2):
        lr, li = p[2 * dr:2 * dr + 1], p[2 * dr + 1:2 * dr + 2]
        dt = jnp.exp(p[4 + dr:5 + dr])
        mag = jnp.exp(lr * dt)
        ar, ai = mag * jnp.cos(li * dt), mag * jnp.sin(li * dt)
        den = lr * lr + li * li
        nr, ni = ar - 1.0, ai
        fr = (nr * lr + ni * li) / den
        fi = (ni * lr - nr * li) / den
        bbr = fr * btr - fi * bti
        bbi = fr * bti + fi * btr
        pr = jnp.ones_like(ar)
        pi = jnp.zeros_like(ar)
        for l in range(chunk + 1):
            wre_ref[0, dr, l] = cr * pr - ci * pi
            wim_ref[0, dr, l] = -(cr * pi + ci * pr)
            if l < chunk:
                ere_ref[0, dr, l] = bbr * pr - bbi * pi
                eim_ref[0, dr, l] = bbr * pi + bbi * pr
            else:
                apr_ref[0, dr] = jnp.broadcast_to(pr, apr_ref.shape[2:])
                api_ref[0, dr] = jnp.broadcast_to(pi, api_ref.shape[2:])
            pr, pi = pr * ar - pi * ai, pr * ai + pi * ar
        nrow = chunk * cr.shape[0]
        wre = wre_ref[0, dr, 0:chunk].reshape(nrow, cr.shape[1])
        wim = wim_ref[0, dr, 0:chunk].reshape(nrow, cr.shape[1])
        k_ref[0, dr] = _dot3(wre, bbr, _dot_nt) + _dot3(wim, bbi, _dot_nt)


def _s5_prep_call(p, btr, bti, cr, ci, *, chunk):
    g, ch, st = cr.shape
    blk = lambda *s: pl.BlockSpec((1,) + s, lambda i: (i,) + (0,) * len(s))
    shp = lambda *s: jax.ShapeDtypeStruct((g,) + s, F32)
    return pl.pallas_call(
        functools.partial(_s5_prep_kernel, chunk=chunk),
        grid=(g,),
        in_specs=[blk(8, st), blk(ch, st), blk(ch, st), blk(ch, st), blk(ch, st)],
        out_specs=[blk(2, chunk + 1, ch, st), blk(2, chunk + 1, ch, st),
                   blk(2, chunk, ch, st), blk(2, chunk, ch, st),
                   blk(2, chunk * ch, ch), blk(2, 8, st), blk(2, 8, st)],
        out_shape=[shp(2, chunk + 1, ch, st), shp(2, chunk + 1, ch, st),
                   shp(2, chunk, ch, st), shp(2, chunk, ch, st),
                   shp(2, chunk * ch, ch), shp(2, 8, st), shp(2, 8, st)],
        compiler_params=_params("parallel"),
        name="s5_prep",
    )(p, btr, bti, cr, ci)


def _s5_kernel(u_ref, m_ref, e_ref, v_ref, ar_ref, ai_ref, o_ref, s_sc, xr_sc, xi_sc,
               *, n_chunks, ctx_chunks, bsz, state):
    u = u_ref[0]
    s_sc[...] = _dot(u, e_ref[0])
    ar, ai = ar_ref[0], ai_ref[0]
    isf = _iota((bsz, 2 * state), 1) < state
    order_b = list(range(ctx_chunks - 1, -1, -1)) + list(range(n_chunks - 1, ctx_chunks - 1, -1))
    xr = jnp.zeros((bsz, 2 * state), F32)
    xi = jnp.zeros((bsz, 2 * state), F32)
    for i in range(n_chunks):
        rf = slice(i * bsz, (i + 1) * bsz)
        rb = slice(order_b[i] * bsz, (order_b[i] + 1) * bsz)
        xr_sc[rf, 0:state] = xr[:, 0:state]
        xi_sc[rf, 0:state] = xi[:, 0:state]
        xr_sc[rb, state:2 * state] = xr[:, state:2 * state]
        xi_sc[rb, state:2 * state] = xi[:, state:2 * state]
        sr = jnp.where(isf, s_sc[rf, 0:2 * state], s_sc[rb, 0:2 * state])
        si = jnp.where(isf, s_sc[rf, 2 * state:4 * state], s_sc[rb, 2 * state:4 * state])
        xr, xi = ar * xr - ai * xi + sr, ar * xi + ai * xr + si
    y = _dot(u, m_ref[0])
    y += _dot(xr_sc[...].astype(BF16), v_ref[0, 0:2 * state])
    y += _dot(xi_sc[...].astype(BF16), v_ref[0, 2 * state:4 * state])
    o_ref[0] = y.astype(BF16)


def _s5_call(u, m, e, v, ar, ai, *, n_chunks, ctx_chunks, bsz):
    g, rows, width = u.shape
    state = ar.shape[-1] // 2
    blk = lambda *s: pl.BlockSpec((1,) + s, lambda i: (i,) + (0,) * len(s))
    return pl.pallas_call(
        functools.partial(_s5_kernel, n_chunks=n_chunks, ctx_chunks=ctx_chunks, bsz=bsz, state=state),
        grid=(g,),
        in_specs=[blk(rows, width), blk(width, width), blk(width, 4 * state), blk(4 * state, width),
                  blk(bsz, 2 * state), blk(bsz, 2 * state)],
        out_specs=blk(rows, width),
        out_shape=jax.ShapeDtypeStruct((g, rows, width), BF16),
        scratch_shapes=[pltpu.VMEM((rows, 4 * state), F32), pltpu.VMEM((rows, 2 * state), F32),
                        pltpu.VMEM((rows, 2 * state), F32)],
        compiler_params=_params("parallel"),
        name="s5",
    )(u, m, e, v, ar, ai)


def _s5_operands(prep, *, chunk):
    wre, wim, ere, eim, kk, apr, api = prep
    g, _, _, ch, st = wre.shape
    kf = kk[:, 0].reshape(g, chunk, ch, ch)
    kb = kk[:, 1].reshape(g, chunk, ch, ch)
    s_i = jnp.arange(chunk)[:, None]
    t_i = jnp.arange(chunk)[None, :]
    tf = jnp.where((t_i >= s_i)[None, :, :, None, None], kf[:, jnp.maximum(t_i - s_i, 0)], 0.0)
    tb = jnp.where((s_i >= t_i)[None, :, :, None, None], kb[:, jnp.maximum(s_i - t_i, 0)], 0.0)
    m = (tf + tb).transpose(0, 1, 4, 2, 3).reshape(g, chunk * ch, chunk * ch).astype(BF16)
    rows = lambda x: x.reshape(g, chunk * ch, st)
    e = jnp.concatenate([rows(ere[:, 0, ::-1]), rows(ere[:, 1]),
                         rows(eim[:, 0, ::-1]), rows(eim[:, 1])], axis=-1).astype(BF16)
    cols = lambda x: x.transpose(0, 3, 1, 2).reshape(g, st, chunk * ch)
    v = jnp.concatenate([cols(wre[:, 0, 1:]), cols(wre[:, 1, :0:-1]),
                         cols(wim[:, 0, 1:]), cols(wim[:, 1, :0:-1])], axis=1).astype(BF16)
    ar = jnp.concatenate([apr[:, 0], apr[:, 1]], axis=-1)
    ai = jnp.concatenate([api[:, 0], api[:, 1]], axis=-1)
    return m, e, v, ar, ai


def _shift(x, n):
    return lax.shift_right_logical(x, jnp.int32(int(math.log2(n))))


def _scan_core(load_chunk, o_ref, st_ref, *, heads, dk, dv, chunk, sub, block):
    kw, vw = heads * dk, heads * dv
    fwd = pl.program_id(1) == 0
    sign = jnp.where(fwd, 1, -1)
    n_chunks = block // chunk

    @pl.when(pl.program_id(2) == 0)
    def _():
        st_ref[...] = jnp.zeros_like(st_ref)

    tri = ((_iota((chunk, chunk), 0) - _iota((chunk, chunk), 1)) * sign >= 0).astype(BF16)
    k_head = _shift(_iota((1, kw), 1), dk)
    v_head = _shift(_iota((1, vw), 1), dv)
    st_mask = _shift(_iota((vw, kw), 0), dv) == _shift(_iota((vw, kw), 1), dk)
    t_idx = _iota((chunk, heads * sub), 0)
    s_off = _iota((chunk, heads * sub), 1) & (sub - 1)

    for i in range(n_chunks):
        ci = jnp.where(fwd, i, n_chunks - 1 - i)
        r0 = pl.multiple_of(ci * chunk, chunk)
        q, k, v, g = load_chunk(r0)
        g_hi, g_lo = _split(g)
        cum = _dot(tri, g_hi) + _dot(tri, g_lo)
        last = jnp.where(fwd, cum[chunk - 1:chunk], cum[0:1])
        state = st_ref[...]
        o = _dot_nt((q * jnp.exp(cum)).astype(BF16), state.astype(BF16))
        for jb in range(chunk // sub):
            lo, hi = jb * sub, (jb + 1) * sub
            a = jnp.where(fwd, cum[hi - 1:hi], cum[lo:lo + 1])
            qa = (q * jnp.exp(jnp.minimum(cum - a, EXP_CLAMP))).astype(BF16)
            ka = k[lo:hi] * jnp.exp(a - cum[lo:hi])
            kbd = jnp.concatenate([jnp.where(k_head == h, ka, 0.0) for h in range(heads)], axis=0)
            p = _dot_nt(qa, kbd.astype(BF16))
            p = jnp.where((t_idx - (lo + s_off)) * sign >= 0, p, 0.0)
            vbd = jnp.concatenate([jnp.where(v_head == h, v[lo:hi], 0.0) for h in range(heads)], axis=0)
            o += _dot(p.astype(BF16), vbd.astype(BF16))
        kd = (k * jnp.exp(last - cum)).astype(BF16)
        upd = _dot_tn(v.astype(BF16), kd)
        st_ref[...] = jnp.where(st_mask, state * jnp.exp(last) + upd, 0.0)
        o_ref[0, pl.ds(r0, chunk), :] = o.astype(o_ref.dtype)


def _gla_scan_kernel(q_ref, k_ref, v_ref, code_ref, gw_ref, gb_ref, o_ref, st_ref, **kw):
    def load_chunk(r0):
        rows = pl.ds(r0, kw["chunk"])
        z = _dot(code_ref[rows, :], gw_ref[0]) + gb_ref[0]
        g = (jnp.minimum(z, 0.0) - jnp.log(1.0 + jnp.exp(-jnp.abs(z)))) * (1.0 / GLA_GATE_NORM)
        q = q_ref[rows, :].astype(F32) * (kw["dk"] ** -0.5)
        return q, k_ref[rows, :].astype(F32), v_ref[rows, :].astype(F32), g
    _scan_core(load_chunk, o_ref, st_ref, **kw)


def _hgrn_scan_kernel(q_ref, f_ref, v_ref, lb_ref, o_ref, st_ref, **kw):
    def load_chunk(r0):
        rows = pl.ds(r0, kw["chunk"])
        lb = lb_ref[...]
        fr = f_ref[rows, :].astype(F32)
        f = lb + (1.0 - lb) * _sigmoid(fr)
        g = jnp.log(jnp.maximum(f, TINY))
        k = (1.0 - lb) * _sigmoid(-fr)
        return _silu(q_ref[rows, :].astype(F32)), k, v_ref[rows, :].astype(F32), g
    _scan_core(load_chunk, o_ref, st_ref, **kw)


def _scan_call(tk, kernel, act, col_specs, extra, extra_specs, *, heads, dk, dv):
    block = SCAN_BLOCK
    ctx_blocks = tk.n_ctx // block
    n_blocks = ctx_blocks + tk.seq // block
    assert tk.n_ctx % block == 0 and tk.seq % block == 0

    def row_block(b, d, j):
        back = jnp.where(j < ctx_blocks, ctx_blocks - 1 - j, n_blocks - 1 + ctx_blocks - j)
        return tk.seq_block(b, jnp.where(d == 0, j, back), block)

    tok = lambda width, col: pl.BlockSpec(
        (block, width), lambda b, d, j, col=col: (row_block(b, d, j), col(d)))
    vw = heads * dv
    return pl.pallas_call(
        functools.partial(kernel, heads=heads, dk=dk, dv=dv, chunk=CHUNK, sub=SCAN_SUB, block=block),
        grid=(tk.bsz, 2, n_blocks),
        in_specs=[tok(w, c) for w, c in col_specs] + extra_specs,
        out_specs=pl.BlockSpec((1, block, vw), lambda b, d, j: (d, row_block(b, d, j), 0)),
        out_shape=jax.ShapeDtypeStruct((2, tk.total, vw), BF16),
        scratch_shapes=[pltpu.VMEM((vw, heads * dk), F32)],
        compiler_params=_params("parallel", "arbitrary", "arbitrary"),
        name=kernel.__name__.strip("_"),
    )(*([act] * len(col_specs)), *extra)


def _head_rms(o, avg_bd, g):
    ms = _dot3(o * o, avg_bd)
    return o * lax.rsqrt(ms + NORM_EPS) * g


def _gelu_tanh(x):
    return 0.5 * x * (1.0 + jnp.tanh(math.sqrt(2.0 / math.pi) * (x + 0.044715 * (x * x * x))))


def _top2(rows):
    m1 = functools.reduce(jnp.maximum, rows)
    first, taken = [], None
    for r in rows:
        hit = r == m1 if taken is None else jnp.logical_and(r == m1, jnp.logical_not(taken))
        first.append(hit)
        taken = hit if taken is None else jnp.logical_or(taken, hit)
    rest = [jnp.where(f, -jnp.inf, r) for f, r in zip(first, rows)]
    m2 = functools.reduce(jnp.maximum, rest)
    second, taken = [], None
    for f, r in zip(first, rest):
        hit = jnp.logical_and(r == m2, jnp.logical_not(f))
        if taken is not None:
            hit = jnp.logical_and(hit, jnp.logical_not(taken))
        second.append(hit)
        taken = hit if taken is None else jnp.logical_or(taken, hit)
    return m1, m2, first, second


def _merge_kernel(act_ref, ys_ref, og_ref, oh_ref, h_ref, m_ref,
                  d_ref, gluw_ref, glub_ref, gng_ref, gavg_ref, hng_ref, havg_ref,
                  wbs_ref, wbg_ref, wbh_ref, wout_ref, fg_ref, rwt_ref, rb_ref,
                  hout_ref, vin_ref, gates_ref):
    m = m_ref[0]
    f32 = lambda c0, w: act_ref[:, c0:c0 + w].astype(F32)

    y = ys_ref[...].astype(F32) + d_ref[...] * f32(COL_U, S5_WIDTH)
    y = _gelu_tanh(y)
    y = y * _sigmoid(_dot(y.astype(BF16), gluw_ref[...]) + glub_ref[...])
    merged = _sigmoid(f32(COL_GATES, D_MODEL)) * _dot(y.astype(BF16), wbs_ref[...])

    o = og_ref[0].astype(F32) + og_ref[1].astype(F32)
    o = _head_rms(o, gavg_ref[...], gng_ref[...]) * _silu(f32(COL_R, GLA_VW))
    merged += _sigmoid(f32(COL_GATES + D_MODEL, D_MODEL)) * _dot(o.astype(BF16), wbg_ref[...])

    o = oh_ref[0].astype(F32) + oh_ref[1].astype(F32)
    o = _head_rms(o, havg_ref[...], hng_ref[...]) * _silu(f32(COL_OG, HGRN_VW))
    merged += _sigmoid(f32(COL_GATES + 2 * D_MODEL, D_MODEL)) * _dot(o.astype(BF16), wbh_ref[...])

    mix = _dot(merged.astype(BF16), wout_ref[...])
    h = h_ref[...] + m[2:3] * mix
    hout_ref[...] = h
    v_in = _rms(h, fg_ref[...]) * (1.0 + m[4:5]) + m[3:4]
    vin_ref[...] = v_in.astype(BF16)

    logits = _dot3(rwt_ref[...], v_in, _dot_nt)
    ex = jnp.exp(logits - jnp.max(logits, axis=0, keepdims=True))
    probs = ex / jnp.sum(ex, axis=0, keepdims=True)
    sel = probs + rb_ref[...]
    rows = [sel[e:e + 1] for e in range(N_EXPERTS)]
    groups = [_top2(rows[g * EXPERTS_PER_GROUP:(g + 1) * EXPERTS_PER_GROUP]) for g in range(N_EXPERT_GROUPS)]
    scores = [g[0] + g[1] for g in groups]
    best = functools.reduce(jnp.maximum, scores)
    taken, chosen = None, []
    for s, (_, _, first, second) in zip(scores, groups):
        in_best = s == best if taken is None else jnp.logical_and(s == best, jnp.logical_not(taken))
        taken = in_best if taken is None else jnp.logical_or(taken, in_best)
        chosen += [jnp.logical_and(in_best, jnp.logical_or(f, c)) for f, c in zip(first, second)]
    e_idx = _iota(probs.shape, 0)
    picked = jnp.zeros_like(probs)
    for e, c in enumerate(chosen):
        picked = jnp.where(jnp.logical_and(e_idx == e, c), probs, picked)
    gates_ref[...] = picked / jnp.sum(picked, axis=0, keepdims=True)


def _merge_call(tk, act, ys, o_gla, o_hg, h, mods, wts, *, tm):
    t, d = h.shape
    tok = lambda w: pl.BlockSpec((tm, w), lambda j: (j, 0))
    two = lambda w: pl.BlockSpec((2, tm, w), lambda j: (0, j, 0))
    full = lambda x: pl.BlockSpec(x.shape, lambda j: (0,) * x.ndim)
    return pl.pallas_call(
        _merge_kernel,
        grid=(t // tm,),
        in_specs=[tok(MERGE_W), tok(S5_WIDTH), two(GLA_VW), two(HGRN_VW), tok(d), tk.mod_spec(tm, d)]
        + [full(w) for w in wts],
        out_specs=[tok(d), tok(d), pl.BlockSpec((N_EXPERTS, tm), lambda j: (0, j))],
        out_shape=[jax.ShapeDtypeStruct((t, d), F32), jax.ShapeDtypeStruct((t, d), BF16),
                   jax.ShapeDtypeStruct((N_EXPERTS, t), F32)],
        compiler_params=_params("parallel"),
        name="merge",
    )(act, ys, o_gla, o_hg, h, mods, *wts)


def _moe_kernel(x_ref, g_ref, wup_ref, wdn_ref, h_ref, m_ref, fg_ref, o_ref, acc_ref, *, final):
    e = pl.program_id(1)

    @pl.when(e == 0)
    def _():
        acc_ref[...] = jnp.zeros_like(acc_ref)

    gu = _dot(x_ref[...], wup_ref[0, 0])
    hid = (_silu(gu[:, :EXPERT_FF]) * gu[:, EXPERT_FF:]).astype(BF16)
    gates = g_ref[...]
    gate = jnp.sum(jnp.where(_iota(gates.shape, 1) == e, gates, 0.0), axis=1, keepdims=True)
    acc_ref[...] += gate * _dot(hid, wdn_ref[0, 0])

    @pl.when(e == pl.num_programs(1) - 1)
    def _():
        h = h_ref[...] + m_ref[0][5:6] * acc_ref[...]
        o_ref[...] = _rms(h, fg_ref[...]) if final else h


def _moe_call(tk, v_in, gates, w_up, w_dn, h, mods, final_g, layer, *, tm, final):
    t, d = h.shape
    ne = gates.shape[-1]
    tok = lambda w: pl.BlockSpec((tm, w), lambda j, e: (j, 0))
    return pl.pallas_call(
        functools.partial(_moe_kernel, final=final),
        grid=(t // tm, ne),
        in_specs=[tok(d), tok(ne),
                  pl.BlockSpec((1, 1) + w_up.shape[2:], lambda j, e: (layer, e, 0, 0)),
                  pl.BlockSpec((1, 1) + w_dn.shape[2:], lambda j, e: (layer, e, 0, 0)),
                  tok(d), tk.mod_spec(tm, d), pl.BlockSpec((1, d), lambda j, e: (0, 0))],
        out_specs=tok(d),
        out_shape=jax.ShapeDtypeStruct((t, d), F32),
        scratch_shapes=[pltpu.VMEM((tm, d), F32)],
        compiler_params=_params("parallel", "arbitrary"),
        name="moe",
    )(v_in, gates, w_up, w_dn, h, mods, final_g)


def _latent_swap(t, split, bsz, a, b):
    lead, w = t.shape[:-2], t.shape[-1]
    lat = t[..., split:, :].reshape(lead + (bsz, a, b, w)).swapaxes(-2, -3).reshape(lead + (-1, w))
    return jnp.concatenate([t[..., :split, :], lat], axis=-2)


def _block_avg(width, head):
    i = jnp.arange(width) // head
    return (i[:, None] == i[None, :]).astype(F32) / head


def _reorder_w_in(w):
    splits = (S5_WIDTH, GLA_KW, GLA_KW, GLA_VW, GLA_VW, 2 * GLA_GATE_RANK, HGRN_KW, 2 * HGRN_KW,
              HGRN_VW, HGRN_VW, 3 * D_MODEL)
    parts, start = [], 0
    for s in splits:
        parts.append(w[:, start:start + s])
        start += s
    u, gq, gk, gv, gr, code, hq, hf, hi, hog, gates = parts
    pad = jnp.zeros((w.shape[0], LANES - code.shape[1]), w.dtype)
    return jnp.concatenate([gr, hog, u, gates, gq, gk, gv, hq, hf, hi, code, pad], axis=1).astype(BF16)


def kernel(x, c, ctx, c_ctx, w_mod, b_mod, norm_mix_g, norm_ffn_g, w_in,
           s5_lam_re, s5_lam_im, s5_log_step, s5_b_re, s5_b_im, s5_c_re, s5_c_im,
           s5_d, s5_glu_w, s5_glu_b, gla_gate_up, gla_gate_b, gla_norm_g,
           hgrn_lower, hgrn_norm_g, w_branch_s5, w_branch_gla, w_branch_hgrn, w_out,
           router_w, router_b, moe_w_up, moe_w_down, final_norm_g):
    bsz, seq, d = x.shape
    n_ctx = ctx.shape[1]
    n = n_ctx + seq
    depth = w_mod.shape[0]
    tk = _Tokens(bsz, n_ctx, seq)
    split = bsz * n_ctx
    n_chunks, ctx_chunks = n // CHUNK, n_ctx // CHUNK
    row = lambda v: v.reshape(1, -1).astype(F32)

    cc = jnp.concatenate([c, c_ctx[None], jnp.zeros((16 - bsz - 1, d), F32)], axis=0)
    mods = _mod_call(cc, w_mod, b_mod).reshape(depth, 16, 6, d)[:, :bsz + 1]
    mods = jnp.pad(mods, ((0, 0), (0, 0), (0, MOD_ROWS - 6), (0, 0)))

    p_lb = jax.nn.softmax(hgrn_lower.astype(F32), axis=0)
    lower_bounds = jnp.cumsum(p_lb, axis=0) - p_lb[0]

    gavg, havg = _block_avg(GLA_VW, GLA_DV), _block_avg(HGRN_VW, HGRN_DV)
    rwt = router_w.T.astype(F32)
    rb_col = router_b.reshape(-1, 1).astype(F32)
    w_up_bf, w_dn_bf = moe_w_up.astype(BF16), moe_w_down.astype(BF16)

    h = jnp.concatenate([ctx.reshape(split, d), x.reshape(bsz * seq, d)], axis=0)
    for layer in range(depth):
        act = _proj_in_call(tk, h, mods[layer], row(norm_mix_g[layer]), _reorder_w_in(w_in[layer]),
                            tm=TOKEN_TILE)

        step = jnp.broadcast_to(s5_log_step[layer][:, :, None], s5_lam_re[layer].shape)
        p = jnp.stack([s5_lam_re[layer, 0], s5_lam_im[layer, 0], s5_lam_re[layer, 1], s5_lam_im[layer, 1],
                       step[0], step[1], jnp.zeros_like(step[0]), jnp.zeros_like(step[0])], axis=1)
        prep = _s5_prep_call(p.astype(F32), s5_b_re[layer].swapaxes(1, 2), s5_b_im[layer].swapaxes(1, 2),
                             s5_c_re[layer], s5_c_im[layer], chunk=CHUNK)
        u_g = (tk.to_sequences(act[:, COL_U:COL_U + S5_WIDTH])
               .reshape(bsz, n_chunks, CHUNK, S5_GROUPS, S5_GROUP_CH)
               .transpose(3, 1, 0, 2, 4).reshape(S5_GROUPS, n_chunks * bsz, CHUNK * S5_GROUP_CH))
        y_g = _s5_call(u_g, *_s5_operands(prep, chunk=CHUNK),
                       n_chunks=n_chunks, ctx_chunks=ctx_chunks, bsz=bsz)
        y_ssm = tk.from_sequences(y_g.reshape(S5_GROUPS, n_chunks, bsz, CHUNK, S5_GROUP_CH)
                                  .transpose(2, 1, 3, 0, 4).reshape(bsz, n, S5_WIDTH))

        gw = jnp.zeros((2, LANES, GLA_KW), F32)
        gw = gw.at[0, :GLA_GATE_RANK].set(gla_gate_up[layer, 0])
        gw = gw.at[1, GLA_GATE_RANK:2 * GLA_GATE_RANK].set(gla_gate_up[layer, 1]).astype(BF16)
        c0 = COL_GLA // GLA_KW
        o_gla = _scan_call(
            tk, _gla_scan_kernel, act,
            [(GLA_KW, lambda dr: c0), (GLA_KW, lambda dr: c0 + 1),
             (GLA_VW, lambda dr: (COL_GLA + 2 * GLA_KW) // GLA_VW), (LANES, lambda dr: COL_CODE // LANES)],
            [gw, gla_gate_b[layer].reshape(2, 1, GLA_KW).astype(F32)],
            [pl.BlockSpec((1, LANES, GLA_KW), lambda b, dr, j: (dr, 0, 0)),
             pl.BlockSpec((1, 1, GLA_KW), lambda b, dr, j: (dr, 0, 0))],
            heads=GLA_HEADS, dk=GLA_DK, dv=GLA_DV)

        hg = _latent_swap(act[:, COL_HG:COL_HG + 4 * HGRN_KW], split, bsz, seq // GRID_W, GRID_W)
        o_hg = _scan_call(
            tk, _hgrn_scan_kernel, hg,
            [(HGRN_KW, lambda dr: 0), (HGRN_KW, lambda dr: 1 + dr), (HGRN_VW, lambda dr: 3)],
            [row(lower_bounds[layer])],
            [pl.BlockSpec((1, HGRN_KW), lambda b, dr, j: (0, 0))],
            heads=HGRN_HEADS, dk=HGRN_EXPAND, dv=HGRN_DV)
        o_hg = _latent_swap(o_hg, split, bsz, GRID_W, seq // GRID_W)

        wts = [row(s5_d[layer]), s5_glu_w[layer].astype(BF16), row(s5_glu_b[layer]),
               row(jnp.tile(gla_norm_g[layer], GLA_HEADS)), gavg,
               row(jnp.tile(hgrn_norm_g[layer], HGRN_HEADS)), havg,
               w_branch_s5[layer].astype(BF16), w_branch_gla[layer].astype(BF16),
               w_branch_hgrn[layer].astype(BF16), w_out[layer].astype(BF16),
               row(norm_ffn_g[layer]), rwt, rb_col]
        h, v_in, gates_t = _merge_call(tk, act, y_ssm, o_gla, o_hg, h, mods[layer], wts, tm=TOKEN_TILE)

        h = _moe_call(tk, v_in, gates_t.T, w_up_bf, w_dn_bf, h, mods[layer], row(final_norm_g), layer,
                      tm=MOE_TILE, final=layer == depth - 1)
    return h[split:].reshape(bsz, seq, d)
```
